```python
import math
import jax
import jax.numpy as jnp
from jax import lax
import numpy as np

D_MODEL = 1024
BATCH = 2
SEQ = 8192
DEPTH = 4

GRID_W = 64
CTX_LEN = 256
N_MIXERS = 3
N_A_LAYERS = (DEPTH + 2) // 3
N_B_LAYERS = (DEPTH + 1) // 3
N_C_LAYERS = DEPTH // 3
D_FF = 2816
N_MOD = 9
RMS_EPS = 1e-6
ROPE_BASE = 10000.0
Q_BLOCK = 128

A_HEADS = 8
A_Q_LORA = 256
A_KV_LORA = 128
A_NOPE = 128
A_ROPE = 64
A_V = 128
A_IN = A_Q_LORA + A_KV_LORA + A_ROPE

B_HEADS = 8
B_HEAD = D_MODEL // (2 * B_HEADS)

C_HEADS = 16
C_HEAD = D_MODEL // C_HEADS
C_WIN_H = 8
C_WIN_W = 16

kernel_name = 'hybrid_mla_diff_natten_macaron_dit'


def rmsnorm(x, g):
    xf = x.astype(jnp.float32)
    y = xf * lax.rsqrt(jnp.mean(xf * xf, axis=-1, keepdims=True) + RMS_EPS)
    return (y * g.astype(jnp.float32)).astype(x.dtype)


def modulation(cond, w_mod, b_mod):
    m = jax.nn.silu(cond) @ w_mod + b_mod
    return m.reshape(cond.shape[0], N_MOD, D_MODEL)


def modulate(x, g, shift, scale):
    return rmsnorm(x, g) * (1.0 + scale[:, None]) + shift[:, None]


def swiglu(h, w_gate, w_up, w_down):
    return (jax.nn.silu(h @ w_gate) * (h @ w_up)) @ w_down


def macaron_half(s, m, base, g, w_gate, w_up, w_down):
    h = modulate(s, g, m[:, base], m[:, base + 1])
    return s + 0.5 * m[:, base + 2, None] * swiglu(h, w_gate, w_up, w_down)


def axial_rope_angles(n_tokens, rot_dim):
    t = jnp.arange(n_tokens, dtype=jnp.int32)
    row = (t // GRID_W).astype(jnp.float32)
    col = (t % GRID_W).astype(jnp.float32)
    axis_dim = rot_dim // 2
    inv = ROPE_BASE ** (-jnp.arange(0, axis_dim, 2, dtype=jnp.float32) / axis_dim)
    ang = jnp.concatenate([row[:, None] * inv, col[:, None] * inv], axis=-1)
    return jnp.cos(ang), jnp.sin(ang)


def apply_rope(x, cos, sin):
    half = x.shape[-1] // 2
    shape = (1, cos.shape[0], 1, half)
    c, s = cos.reshape(shape), sin.reshape(shape)
    xf = x.astype(jnp.float32)
    x1, x2 = xf[..., :half], xf[..., half:]
    return jnp.concatenate([x1 * c - x2 * s, x1 * s + x2 * c], axis=-1).astype(x.dtype)


def attend(q, k, v, scale):
    s = jnp.einsum('bqhd,bkhd->bhqk', q, k) * scale
    p = jax.nn.softmax(s.astype(jnp.float32), axis=-1).astype(v.dtype)
    return jnp.einsum('bhqk,bkhd->bqhd', p, v)


def query_blocks(fn, q):
    b, n = q.shape[:2]
    nb = n // Q_BLOCK
    qb = jnp.moveaxis(q.reshape((b, nb, Q_BLOCK) + q.shape[2:]), 1, 0)
    out = lax.map(fn, qb)
    return jnp.moveaxis(out, 0, 1).reshape((b, n) + out.shape[3:])


def heads(t, n_heads, head_dim):
    return t.reshape(t.shape[0], t.shape[1], n_heads, head_dim)


def qkv_proj(h, w_qkv, need_q):
    if need_q:
        q, k, v = jnp.split(h @ w_qkv, 3, axis=-1)
        return q, k, v
    k, v = jnp.split(h @ w_qkv[:, D_MODEL:], 2, axis=-1)
    return None, k, v


def mla_project(h, w_in, g_q, g_kv, w_uq, w_ukv, rope, need_q):
    b, n, _ = h.shape
    if need_q:
        cq, ckv, k_pe = jnp.split(h @ w_in, [A_Q_LORA, A_Q_LORA + A_KV_LORA], axis=-1)
    else:
        ckv, k_pe = jnp.split(h @ w_in[:, A_Q_LORA:], [A_KV_LORA], axis=-1)
    kv = heads(rmsnorm(ckv, g_kv) @ w_ukv, A_HEADS, A_NOPE + A_V)
    k_pe = k_pe[:, :, None, :]
    if rope is not None:
        k_pe = apply_rope(k_pe, *rope)
    k = jnp.concatenate([kv[..., :A_NOPE], jnp.broadcast_to(k_pe, (b, n, A_HEADS, A_ROPE))], axis=-1)
    v = kv[..., A_NOPE:]
    q = None
    if need_q:
        qh = heads(rmsnorm(cq, g_q) @ w_uq, A_HEADS, A_NOPE + A_ROPE)
        q_pe = qh[..., A_NOPE:]
        if rope is not None:
            q_pe = apply_rope(q_pe, *rope)
        q = jnp.concatenate([qh[..., :A_NOPE], q_pe], axis=-1)
    return q, k, v


def mla_mixer(h_lat, h_ctx, w_in, g_q, g_kv, w_uq, w_ukv, w_out, need_ctx_out):
    b, n, _ = h_lat.shape
    scale = (A_NOPE + A_ROPE) ** -0.5
    rope = axial_rope_angles(n, A_ROPE)
    q_l, k_l, v_l = mla_project(h_lat, w_in, g_q, g_kv, w_uq, w_ukv, rope, True)
    q_c, k_c, v_c = mla_project(h_ctx, w_in, g_q, g_kv, w_uq, w_ukv, None, need_ctx_out)
    k_all = jnp.concatenate([k_c, k_l], axis=1)
    v_all = jnp.concatenate([v_c, v_l], axis=1)
    o_l = query_blocks(lambda qb: attend(qb, k_all, v_all, scale), q_l)
    out_lat = o_l.reshape(b, n, A_HEADS * A_V) @ w_out
    out_ctx = None
    if need_ctx_out:
        o_c = attend(q_c, k_c, v_c, scale)
        out_ctx = o_c.reshape(b, h_ctx.shape[1], A_HEADS * A_V) @ w_out
    return out_lat, out_ctx


def diff_heads(h, w_qkv, rope, need_q):
    q, k, v = qkv_proj(h, w_qkv, need_q)
    k = heads(k, 2 * B_HEADS, B_HEAD)
    v = heads(v, B_HEADS, 2 * B_HEAD)
    if q is not None:
        q = heads(q, 2 * B_HEADS, B_HEAD)
    if rope is not None:
        q = apply_rope(q, *rope)
        k = apply_rope(k, *rope)
    return q, k, v


def diff_attend(q, k, v, lam, scale):
    b, nq = q.shape[:2]
    s = jnp.einsum('bqhd,bkhd->bhqk', q, k) * scale
    p = jax.nn.softmax(s.astype(jnp.float32), axis=-1).reshape(b, B_HEADS, 2, nq, k.shape[1])
    a = (p[:, :, 0] - lam * p[:, :, 1]).astype(v.dtype)
    return jnp.einsum('bhqk,bkhd->bqhd', a, v)


def diff_mixer(h_lat, h_ctx, w_qkv, lq1, lk1, lq2, lk2, g_sub, w_out, lambda_init, need_ctx_out):
    b, n, _ = h_lat.shape
    scale = B_HEAD ** -0.5
    rope = axial_rope_angles(n, B_HEAD)
    q_l, k_l, v_l = diff_heads(h_lat, w_qkv, rope, True)
    q_c, k_c, v_c = diff_heads(h_ctx, w_qkv, None, need_ctx_out)
    f = jnp.float32
    lam = (jnp.exp(jnp.sum(lq1.astype(f) * lk1.astype(f)))
           - jnp.exp(jnp.sum(lq2.astype(f) * lk2.astype(f))) + lambda_init)
    k_all = jnp.concatenate([k_c, k_l], axis=1)
    v_all = jnp.concatenate([v_c, v_l], axis=1)

    def finish(o):
        o = rmsnorm(o, g_sub) * (1.0 - lambda_init)
        return o.reshape(o.shape[0], o.shape[1], D_MODEL) @ w_out

    out_lat = finish(query_blocks(lambda qb: diff_attend(qb, k_all, v_all, lam, scale), q_l))
    out_ctx = finish(diff_attend(q_c, k_c, v_c, lam, scale)) if need_ctx_out else None
    return out_lat, out_ctx


def na_mixer(h_lat, h_ctx, w_qkv, rpb, w_out, need_ctx_out):
    b, n, _ = h_lat.shape
    rows = n // GRID_W
    kh, kw = min(C_WIN_H, rows), C_WIN_W
    scale = C_HEAD ** -0.5
    q, k, v = qkv_proj(h_lat, w_qkv, True)
    grid = (b, rows, GRID_W, C_HEADS, C_HEAD)
    q_g, k_g, v_g = q.reshape(grid), k.reshape(grid), v.reshape(grid)
    q_c, k_c, v_c = qkv_proj(h_ctx, w_qkv, need_ctx_out)
    k_c, v_c = heads(k_c, C_HEADS, C_HEAD), heads(v_c, C_HEADS, C_HEAD)

    col = np.arange(GRID_W)
    col_start = np.clip(col - kw // 2, 0, GRID_W - kw)
    col_idx = col_start[:, None] + np.arange(kw)[None, :]
    col_bias_idx = col_idx - col[:, None] + (C_WIN_W - 1)

    def row_step(args):
        r, q_r = args
        rs = jnp.clip(r - kh // 2, 0, rows - kh)
        k_band = lax.dynamic_slice_in_dim(k_g, rs, kh, axis=1)
        v_band = lax.dynamic_slice_in_dim(v_g, rs, kh, axis=1)
        k_nb = k_band[:, :, col_idx]
        v_nb = v_band[:, :, col_idx]
        s_nb = jnp.einsum('bwhd,biwjhd->bhwij', q_r, k_nb) * scale
        row_bias_idx = rs + jnp.arange(kh, dtype=jnp.int32) - r + (C_WIN_H - 1)
        bias = rpb[:, row_bias_idx][:, :, col_bias_idx]
        s_nb = s_nb + jnp.transpose(bias, (0, 2, 1, 3))[None]
        s_ctx = jnp.einsum('bwhd,bkhd->bhwk', q_r, k_c) * scale
        s = jnp.concatenate([s_nb.reshape(b, C_HEADS, GRID_W, kh * kw), s_ctx], axis=-1)
        p = jax.nn.softmax(s.astype(jnp.float32), axis=-1).astype(v_g.dtype)
        p_nb = p[..., :kh * kw].reshape(b, C_HEADS, GRID_W, kh, kw)
        p_ctx = p[..., kh * kw:]
        return (jnp.einsum('bhwij,biwjhd->bwhd', p_nb, v_nb)
                + jnp.einsum('bhwk,bkhd->bwhd', p_ctx, v_c))

    o = lax.map(row_step, (jnp.arange(rows, dtype=jnp.int32), jnp.moveaxis(q_g, 1, 0)))
    out_lat = jnp.moveaxis(o, 0, 1).reshape(b, n, D_MODEL) @ w_out
    out_ctx = None
    if need_ctx_out:
        o_c = attend(heads(q_c, C_HEADS, C_HEAD), k_c, v_c, scale)
        out_ctx = o_c.reshape(b, h_ctx.shape[1], D_MODEL) @ w_out
    return out_lat, out_ctx


def setup_inputs(seed: int = 0) -> dict:
    key = jax.random.key(seed)
    ks = iter(jax.random.split(key, 32))
    f32 = jnp.float32

    def w(shape, fan_in):
        return jax.random.normal(next(ks), shape, f32) * (fan_in ** -0.5)

    def gain(shape):
        return 1.0 + 0.02 * jax.random.normal(next(ks), shape, f32)

    def small(shape, std):
        return std * jax.random.normal(next(ks), shape, f32)

    D = D_MODEL
    return {
        'x': jax.random.normal(next(ks), (BATCH, SEQ, D), f32),
        'c': jax.random.normal(next(ks), (BATCH, D), f32),
        'ctx': jax.random.normal(next(ks), (BATCH, CTX_LEN, D), f32),
        'c_ctx': jax.random.normal(next(ks), (D,), f32),
        'w_mod': w((DEPTH, D, N_MOD * D), D),
        'b_mod': small((DEPTH, N_MOD * D), 0.02),
        'norm_g': gain((DEPTH, 3, D)),
        'w_ffn_gate': w((DEPTH, 2, D, D_FF), D),
        'w_ffn_up': w((DEPTH, 2, D, D_FF), D),
        'w_ffn_down': w((DEPTH, 2, D_FF, D), D_FF),
        'a_w_in': w((N_A_LAYERS, D, A_IN), D),
        'a_q_norm': gain((N_A_LAYERS, A_Q_LORA)),
        'a_kv_norm': gain((N_A_LAYERS, A_KV_LORA)),
        'a_w_uq': w((N_A_LAYERS, A_Q_LORA, A_HEADS * (A_NOPE + A_ROPE)), A_Q_LORA),
        'a_w_ukv': w((N_A_LAYERS, A_KV_LORA, A_HEADS * (A_NOPE + A_V)), A_KV_LORA),
        'a_w_out': w((N_A_LAYERS, A_HEADS * A_V, D), A_HEADS * A_V),
        'b_w_qkv': w((N_B_LAYERS, D, 3 * D), D),
        'b_lambda_q1': small((N_B_LAYERS, B_HEAD), 0.1),
        'b_lambda_k1': small((N_B_LAYERS, B_HEAD), 0.1),
        'b_lambda_q2': small((N_B_LAYERS, B_HEAD), 0.1),
        'b_lambda_k2': small((N_B_LAYERS, B_HEAD), 0.1),
        'b_subln': gain((N_B_LAYERS, 2 * B_HEAD)),
        'b_w_out': w((N_B_LAYERS, D, D), D),
        'c_w_qkv': w((N_C_LAYERS, D, 3 * D), D),
        'c_rpb': small((N_C_LAYERS, C_HEADS, 2 * C_WIN_H - 1, 2 * C_WIN_W - 1), 0.02),
        'c_w_out': w((N_C_LAYERS, D, D), D),
        'final_g': gain((D,)),
    }


def reference(x, c, ctx, c_ctx, w_mod, b_mod, norm_g, w_ffn_gate, w_ffn_up, w_ffn_down,
              a_w_in, a_q_norm, a_kv_norm, a_w_uq, a_w_ukv, a_w_out,
              b_w_qkv, b_lambda_q1, b_lambda_k1, b_lambda_q2, b_lambda_k2, b_subln, b_w_out,
              c_w_qkv, c_rpb, c_w_out, final_g):
    s_lat, s_ctx = x, ctx
    for i in range(DEPTH):
        need_ctx_out = i < DEPTH - 1
        m_lat = modulation(c, w_mod[i], b_mod[i])
        m_ctx = modulation(c_ctx[None, :], w_mod[i], b_mod[i])
        ffn0 = (w_ffn_gate[i, 0], w_ffn_up[i, 0], w_ffn_down[i, 0])
        ffn1 = (w_ffn_gate[i, 1], w_ffn_up[i, 1], w_ffn_down[i, 1])

        s_lat = macaron_half(s_lat, m_lat, 0, norm_g[i, 0], *ffn0)
        s_ctx = macaron_half(s_ctx, m_ctx, 0, norm_g[i, 0], *ffn0)

        h_lat = modulate(s_lat, norm_g[i, 1], m_lat[:, 3], m_lat[:, 4])
        h_ctx = modulate(s_ctx, norm_g[i, 1], m_ctx[:, 3], m_ctx[:, 4])
        kind, j = i % N_MIXERS, i // N_MIXERS
        if kind == 0:
            o_lat, o_ctx = mla_mixer(h_lat, h_ctx, a_w_in[j], a_q_norm[j], a_kv_norm[j],
                                     a_w_uq[j], a_w_ukv[j], a_w_out[j], need_ctx_out)
        elif kind == 1:
            lambda_init = 0.8 - 0.6 * math.exp(-0.3 * i)
            o_lat, o_ctx = diff_mixer(h_lat, h_ctx, b_w_qkv[j], b_lambda_q1[j], b_lambda_k1[j],
                                      b_lambda_q2[j], b_lambda_k2[j], b_subln[j], b_w_out[j],
                                      lambda_init, need_ctx_out)
        else:
            o_lat, o_ctx = na_mixer(h_lat, h_ctx, c_w_qkv[j], c_rpb[j], c_w_out[j], need_ctx_out)

        s_lat = s_lat + m_lat[:, 5, None] * o_lat
        s_lat = macaron_half(s_lat, m_lat, 6, norm_g[i, 2], *ffn1)
        if need_ctx_out:
            s_ctx = s_ctx + m_ctx[:, 5, None] * o_ctx
            s_ctx = macaron_half(s_ctx, m_ctx, 6, norm_g[i, 2], *ffn1)
    return rmsnorm(s_lat, final_g)
```

```python
import functools
import math

import jax
import jax.numpy as jnp
from jax import lax
from jax.experimental import pallas as pl
from jax.experimental.pallas import tpu as pltpu

F32 = jnp.float32
BF16 = jnp.bfloat16

D = 1024
B = 2
SEQ = 8192
CTX = 256
T = CTX + SEQ
DEPTH = 4
GRID_W = 64
D_FF = 2816
N_MOD = 9
EPS = 1e-6
ROPE_BASE = 10000.0
LOG2E = math.log2(math.e)
NEG = -1e30

A_HEADS, A_Q_LORA, A_KV_LORA, A_NOPE, A_ROPE, A_V = 8, 256, 128, 128, 64, 128
B_HEADS, B_HEAD = 8, 64
C_HEADS, C_HEAD, C_WIN_H, C_WIN_W = 16, 64, 8, 16

LANES = 128
TM = 256
NT = T // TM
TQ = 256
TK = 512
VMEM_LIMIT = 56 * 1024 * 1024

assert CTX == TM == TQ and SEQ % TK == 0 and SEQ % TM == 0


def _cparams(n_grid):
    return pltpu.CompilerParams(dimension_semantics=("arbitrary",) * n_grid,
                                vmem_limit_bytes=VMEM_LIMIT)


def _dot(a, b):
    return jnp.dot(a, b, preferred_element_type=F32)


def _dot_nt(a, b):
    return lax.dot_general(a, b, (((1,), (1,)), ((), ())), preferred_element_type=F32)


def _rms(x, g):
    return x * lax.rsqrt(jnp.mean(x * x, axis=-1, keepdims=True) + EPS) * g


def _modulated(x, g, shift, scale):
    return _rms(x, g) * (1.0 + scale) + shift


def _silu(x):
    return x * (1.0 / (1.0 + jnp.exp(-x)))


def _mod_row_index(b, t):
    return jnp.where(t == 0, B, b)


def _mod_kernel(c_ref, w_ref, b_ref, o_ref):
    a = _silu(c_ref[...])
    o_ref[0] = jnp.dot(a, w_ref[0], preferred_element_type=F32,
                       precision=lax.Precision.HIGHEST) + b_ref[0]


def _modulation(cond, w_mod, b_mod):
    n_cols = N_MOD * D
    bn = D
    return pl.pallas_call(
        _mod_kernel,
        grid=(DEPTH, n_cols // bn),
        in_specs=[pl.BlockSpec((8, D), lambda l, n: (0, 0)),
                  pl.BlockSpec((1, D, bn), lambda l, n: (l, 0, n)),
                  pl.BlockSpec((1, 1, bn), lambda l, n: (l, 0, n))],
        out_specs=pl.BlockSpec((1, 8, bn), lambda l, n: (l, 0, n)),
        out_shape=jax.ShapeDtypeStruct((DEPTH, 8, n_cols), F32),
        compiler_params=_cparams(2),
        name="modulation",
    )(cond, w_mod, b_mod.reshape(DEPTH, 1, n_cols))


def _ffn_kernel(*refs, base, pre, final):
    refs = list(refs)
    s_ref = refs.pop(0)
    if pre:
        o_ref_in = refs.pop(0)
        wo_ref = refs.pop(0)
    mod_ref, g_ref, wg_ref, wu_ref, wd_ref = refs[:5]
    refs = refs[5:]
    if final:
        fg_ref = refs.pop(0)
    out_ref = refs.pop(0)

    x = s_ref[0]
    mod = mod_ref[0]
    if pre:
        x = x + mod[5:6] * _dot(o_ref_in[0], wo_ref[...])
    h = _modulated(x, g_ref[...], mod[base:base + 1], mod[base + 1:base + 2]).astype(BF16)
    gate = _dot(h, wg_ref[...])
    up = _dot(h, wu_ref[...])
    a = (_silu(gate) * up).astype(BF16)
    y = _dot(a, wd_ref[...])
    out = x + (0.5 * mod[base + 2:base + 3]) * y
    if final:
        out = _rms(out, fg_ref[...])
    out_ref[0] = out


def _ffn(s, mods_l, g, wg, wu, wd, *, base, mix=None, final_g=None):
    pre, final = mix is not None, final_g is not None
    t0 = 1 if final else 0
    resident = functools.partial(pl.BlockSpec, pipeline_mode=pl.Buffered(1))
    tile = pl.BlockSpec((1, TM, D), lambda b, t: (b, t + t0, 0))
    args, specs = [s], [tile]
    if pre:
        args += [mix[0], mix[1]]
        specs += [tile, resident((D, D), lambda b, t: (0, 0))]
    args += [mods_l, g, wg, wu, wd]
    specs += [pl.BlockSpec((1, N_MOD, D), lambda b, t: (_mod_row_index(b, t + t0), 0, 0)),
              pl.BlockSpec((1, D), lambda b, t: (0, 0)),
              resident((D, D_FF), lambda b, t: (0, 0)),
              resident((D, D_FF), lambda b, t: (0, 0)),
              resident((D_FF, D), lambda b, t: (0, 0))]
    if final:
        args.append(final_g)
        specs.append(pl.BlockSpec((1, D), lambda b, t: (0, 0)))
    n_rows = T - t0 * TM
    return pl.pallas_call(
        functools.partial(_ffn_kernel, base=base, pre=pre, final=final),
        grid=(B, NT - t0),
        in_specs=specs,
        out_specs=pl.BlockSpec((1, TM, D), lambda b, t: (b, t, 0)),
        out_shape=jax.ShapeDtypeStruct((B, n_rows, D), F32),
        compiler_params=_cparams(2),
        name="ffn_final" if final else ("ffn_mix" if pre else "ffn"),
    )(*args)


def _rope_tables():
    t = jnp.arange(SEQ, dtype=jnp.int32)
    row = (t // GRID_W).astype(F32)
    col = (t % GRID_W).astype(F32)
    axis_dim = A_ROPE // 2
    inv = ROPE_BASE ** (-jnp.arange(0, axis_dim, 2, dtype=F32) / axis_dim)
    ang = jnp.concatenate([row[:, None] * inv, col[:, None] * inv], axis=-1)
    cos = jnp.concatenate([jnp.ones((CTX, 32), F32), jnp.cos(ang)], axis=0)
    sin = jnp.concatenate([jnp.zeros((CTX, 32), F32), jnp.sin(ang)], axis=0)
    cos = jnp.tile(cos, (1, 4))
    sin = jnp.tile(sin, (1, 4))
    first_half = (jnp.arange(LANES) % 64) < 32
    return cos, jnp.where(first_half, -sin, 0.0), jnp.where(first_half, 0.0, sin)


def _rope_slab(x, cc, sa, sb):
    return x * cc + pltpu.roll(x, 96, 1) * sa + pltpu.roll(x, 32, 1) * sb


def _mla_proj_kernel(s_ref, mod_ref, g_ref, wq_ref, wkv_ref, wpe_ref, gq_ref, gkv_ref,
                     wuqn_ref, wuqp_ref, wuk_ref, wuv_ref, cc_ref, sa_ref, sb_ref,
                     q_ref, k_ref, v_ref, *, q_scale):
    mod = mod_ref[0]
    h = _modulated(s_ref[0], g_ref[...], mod[3:4], mod[4:5]).astype(BF16)
    cc, sa, sb = cc_ref[...], sa_ref[...], sb_ref[...]
    cq = _rms(_dot(h, wq_ref[...]), gq_ref[...]).astype(BF16)
    ckv = _rms(_dot(h, wkv_ref[...]), gkv_ref[...]).astype(BF16)
    kpe = _dot(h, wpe_ref[...])
    kpe = [_rope_slab(kpe[:, i * LANES:(i + 1) * LANES], cc, sa, sb).astype(BF16) for i in range(2)]
    qn = _dot(cq, wuqn_ref[...]) * q_scale
    qp = _dot(cq, wuqp_ref[...])
    qp = [(_rope_slab(qp[:, j * LANES:(j + 1) * LANES], cc, sa, sb) * q_scale).astype(BF16)
          for j in range(A_HEADS // 2)]
    kn = _dot(ckv, wuk_ref[...])
    v_ref[0] = _dot(ckv, wuv_ref[...]).astype(BF16)
    for hd in range(A_HEADS):
        lo = 2 * LANES * hd
        q_ref[0, :, lo:lo + LANES] = qn[:, hd * LANES:(hd + 1) * LANES].astype(BF16)
        q_ref[0, :, lo + LANES:lo + 2 * LANES] = qp[hd // 2]
        k_ref[0, :, lo:lo + LANES] = kn[:, hd * LANES:(hd + 1) * LANES].astype(BF16)
        k_ref[0, :, lo + LANES:lo + 2 * LANES] = kpe[hd % 2]


def _mla_proj(s, mods_l, g, w_in, g_q, g_kv, w_uq, w_ukv, rope):
    wq = w_in[:, :A_Q_LORA].astype(BF16)
    wkv = w_in[:, A_Q_LORA:A_Q_LORA + A_KV_LORA].astype(BF16)
    wpe = w_in[:, A_Q_LORA + A_KV_LORA:]
    z = jnp.zeros_like(wpe)
    wpe = jnp.concatenate([wpe, z, z, wpe], axis=1).astype(BF16)
    wuq = w_uq.reshape(A_Q_LORA, A_HEADS, A_NOPE + A_ROPE)
    wuqn = wuq[:, :, :A_NOPE].reshape(A_Q_LORA, A_HEADS * A_NOPE).astype(BF16)
    wuqp = wuq[:, :, A_NOPE:].reshape(A_Q_LORA, A_HEADS * A_ROPE).astype(BF16)
    wukv = w_ukv.reshape(A_KV_LORA, A_HEADS, A_NOPE + A_V)
    wuk = wukv[:, :, :A_NOPE].reshape(A_KV_LORA, A_HEADS * A_NOPE).astype(BF16)
    wuv = wukv[:, :, A_NOPE:].reshape(A_KV_LORA, A_HEADS * A_V).astype(BF16)
    q_scale = (A_NOPE + A_ROPE) ** -0.5 * LOG2E

    def full(shape):
        return pl.BlockSpec(shape, lambda b, t: (0,) * len(shape))

    tile = lambda w: pl.BlockSpec((1, TM, w), lambda b, t: (b, t, 0))
    tab = pl.BlockSpec((TM, LANES), lambda b, t: (t, 0))
    slot_w = 2 * LANES * A_HEADS
    return pl.pallas_call(
        functools.partial(_mla_proj_kernel, q_scale=q_scale),
        grid=(B, NT),
        in_specs=[tile(D),
                  pl.BlockSpec((1, N_MOD, D), lambda b, t: (_mod_row_index(b, t), 0, 0)),
                  full((1, D)), full(wq.shape), full(wkv.shape), full(wpe.shape),
                  full((1, A_Q_LORA)), full((1, A_KV_LORA)),
                  full(wuqn.shape), full(wuqp.shape), full(wuk.shape), full(wuv.shape),
                  tab, tab, tab],
        out_specs=[tile(slot_w), tile(slot_w), tile(A_HEADS * A_V)],
        out_shape=[jax.ShapeDtypeStruct((B, T, slot_w), BF16),
                   jax.ShapeDtypeStruct((B, T, slot_w), BF16),
                   jax.ShapeDtypeStruct((B, T, A_HEADS * A_V), BF16)],
        compiler_params=_cparams(2),
        name="mla_proj",
    )(s, mods_l, g, wq, wkv, wpe, g_q.reshape(1, -1), g_kv.reshape(1, -1),
      wuqn, wuqp, wuk, wuv, *rope)


def _qkv_proj_kernel(s_ref, mod_ref, g_ref, wq_ref, wk_ref, wv_ref, cc_ref, sa_ref, sb_ref,
                     q_ref, k_ref, v_ref, *, q_scale, rope):
    mod = mod_ref[0]
    h = _modulated(s_ref[0], g_ref[...], mod[3:4], mod[4:5]).astype(BF16)
    q = _dot(h, wq_ref[...])
    k = _dot(h, wk_ref[...])
    v_ref[0] = _dot(h, wv_ref[...]).astype(BF16)
    if rope:
        cc, sa, sb = cc_ref[...], sa_ref[...], sb_ref[...]
        for j in range(D // LANES):
            sl = slice(j * LANES, (j + 1) * LANES)
            q_ref[0, :, sl] = (_rope_slab(q[:, sl], cc, sa, sb) * q_scale).astype(BF16)
            k_ref[0, :, sl] = _rope_slab(k[:, sl], cc, sa, sb).astype(BF16)
    else:
        q_ref[0] = (q * q_scale).astype(BF16)
        k_ref[0] = k.astype(BF16)


def _qkv_proj(s, mods_l, g, w_qkv, rope_tabs, *, q_scale, rope):
    wq = w_qkv[:, :D].astype(BF16)
    wk = w_qkv[:, D:2 * D].astype(BF16)
    wv = w_qkv[:, 2 * D:].astype(BF16)
    full = lambda shape: pl.BlockSpec(shape, lambda b, t: (0,) * len(shape))
    tile = pl.BlockSpec((1, TM, D), lambda b, t: (b, t, 0))
    tab = pl.BlockSpec((TM, LANES), lambda b, t: (t, 0))
    return pl.pallas_call(
        functools.partial(_qkv_proj_kernel, q_scale=q_scale, rope=rope),
        grid=(B, NT),
        in_specs=[tile, pl.BlockSpec((1, N_MOD, D), lambda b, t: (_mod_row_index(b, t), 0, 0)),
                  full((1, D)), full((D, D)), full((D, D)), full((D, D)), tab, tab, tab],
        out_specs=[tile, tile, tile],
        out_shape=[jax.ShapeDtypeStruct((B, T, D), BF16)] * 3,
        compiler_params=_cparams(2),
        name="qkv_proj_rope" if rope else "qkv_proj",
    )(s, mods_l, g, wq, wk, wv, *rope_tabs)


def _attn_kernel(*refs, mode, lambda_init):
    if mode == "diff":
        q_ref, k_ref, v_ref, lam_ref, gsub_ref, o_ref = refs
    else:
        q_ref, k_ref, v_ref, o_ref = refs
    qi = pl.program_id(2)
    q = q_ref[0]
    if mode == "diff":
        lane = lax.broadcasted_iota(jnp.int32, q.shape, 1)
        zero = jnp.zeros_like(q)
        q = jnp.concatenate([jnp.where(lane < B_HEAD, q, zero),
                             jnp.where(lane >= B_HEAD, q, zero)], axis=0)
    rows = q.shape[0]
    dv = v_ref.shape[-1]

    def step(start, size, carry):
        m, l, acc = carry
        k = k_ref[0, pl.ds(start, size), :]
        v = v_ref[0, pl.ds(start, size), :]
        s = _dot_nt(q, k)
        m_new = jnp.maximum(m, jnp.max(s, axis=-1, keepdims=True))
        alpha = jnp.exp2(m - m_new)
        p = jnp.exp2(s - m_new)
        l = alpha * l + jnp.sum(p, axis=-1, keepdims=True)
        acc = alpha * acc + _dot(p.astype(BF16), v)
        return m_new, l, acc

    carry = (jnp.full((rows, 1), -jnp.inf, F32), jnp.zeros((rows, 1), F32),
             jnp.zeros((rows, dv), F32))
    carry = step(0, CTX, carry)
    n_latent_steps = jnp.where(qi == 0, 0, SEQ // TK)
    _, l, acc = lax.fori_loop(
        0, n_latent_steps,
        lambda c, cr: step(pl.multiple_of(CTX + c * TK, TM), TK, cr), carry)
    o = acc * (1.0 / l)
    if mode == "diff":
        f = lam_ref[...]
        lam = (jnp.exp(jnp.sum(f[0:1] * f[1:2], axis=-1, keepdims=True))
               - jnp.exp(jnp.sum(f[2:3] * f[3:4], axis=-1, keepdims=True)) + lambda_init)
        half = rows // 2
        o = o[:half] - lam * o[half:]
        o = _rms(o, gsub_ref[...]) * (1.0 - lambda_init)
    o_ref[0] = o.astype(BF16)


def _attention(q, k, v, *, mode, n_heads, dk, lam=None, gsub=None, lambda_init=0.0):
    dv = LANES
    args = [q, k, v]
    specs = [pl.BlockSpec((1, TQ, dk), lambda b, h, i: (b, i, h)),
             pl.BlockSpec((1, T, dk), lambda b, h, i: (b, 0, h)),
             pl.BlockSpec((1, T, dv), lambda b, h, i: (b, 0, h))]
    if mode == "diff":
        args += [lam, gsub]
        specs += [pl.BlockSpec(lam.shape, lambda b, h, i: (0, 0)),
                  pl.BlockSpec(gsub.shape, lambda b, h, i: (0, 0))]
    return pl.pallas_call(
        functools.partial(_attn_kernel, mode=mode, lambda_init=lambda_init),
        grid=(B, n_heads, T // TQ),
        in_specs=specs,
        out_specs=pl.BlockSpec((1, TQ, dv), lambda b, h, i: (b, i, h)),
        out_shape=jax.ShapeDtypeStruct((B, T, n_heads * dv), BF16),
        compiler_params=_cparams(3),
        name="attn_" + mode,
    )(*args)


N_ROWS = SEQ // GRID_W
CTX_ROWS = CTX // GRID_W
BAND = C_WIN_H * GRID_W
N_BIAS_TYPES = C_WIN_H + 1
N_PAIRS = C_HEADS // 2
RPB_H, RPB_W = 2 * C_WIN_H - 1, 2 * C_WIN_W - 1


def _band_start(rr):
    return jnp.clip(rr - C_WIN_H // 2, 0, N_ROWS - C_WIN_H)


def _na_bias_kernel(rpb_ref, o_ref):
    a = pl.program_id(0)
    j = pl.program_id(1)
    a_eff = jnp.minimum(a, C_WIN_H - 1)
    shape = (GRID_W, LANES)
    c = lax.broadcasted_iota(jnp.int32, shape, 0)
    lane = lax.broadcasted_iota(jnp.int32, shape, 1)
    kc = lane % GRID_W
    hi_lane = lane >= GRID_W
    diff = kc - c + (C_WIN_W - 1)
    cs = jnp.clip(c - C_WIN_W // 2, 0, GRID_W - C_WIN_W)
    valid = (kc >= cs) & (kc < cs + C_WIN_W) & (a < C_WIN_H)
    for hh in range(2):
        head = 2 * j + hh
        for u in range(C_WIN_H // 2):
            dr_lo = (C_WIN_H - 1) - a_eff + 2 * u
            base_lo = (head * RPB_H + dr_lo) * RPB_W
            blk = jnp.zeros(shape, F32)
            for jj in range(RPB_W):
                val = jnp.where(hi_lane, rpb_ref[base_lo + RPB_W + jj], rpb_ref[base_lo + jj])
                blk = jnp.where(diff == jj, val, blk)
            blk = jnp.where(valid, blk * LOG2E, NEG)
            o_ref[0, 0, hh * GRID_W:(hh + 1) * GRID_W, u * LANES:(u + 1) * LANES] = blk


def _na_bias(rpb):
    return pl.pallas_call(
        _na_bias_kernel,
        grid=(N_BIAS_TYPES, N_PAIRS),
        in_specs=[pl.BlockSpec(memory_space=pltpu.SMEM)],
        out_specs=pl.BlockSpec((1, 1, 2 * GRID_W, BAND), lambda a, j: (a, j, 0, 0)),
        out_shape=jax.ShapeDtypeStruct((N_BIAS_TYPES, N_PAIRS, 2 * GRID_W, BAND), F32),
        compiler_params=_cparams(2),
        name="na_bias",
    )(rpb.reshape(-1))


def _na_kernel(q_ref, kb_ref, vb_ref, kc_ref, vc_ref, bias_ref, o_ref):
    q = q_ref[0, 0]
    kb = kb_ref[0].reshape(BAND, D)
    vb = vb_ref[0].reshape(BAND, D)
    kc = kc_ref[0].reshape(CTX, D)
    vc = vc_ref[0].reshape(CTX, D)
    lane = lax.broadcasted_iota(jnp.int32, (GRID_W, LANES), 1)
    lo = lane < C_HEAD
    for j in range(N_PAIRS):
        sl = slice(j * LANES, (j + 1) * LANES)
        qp = q[:, sl]
        zero = jnp.zeros_like(qp)
        qs = jnp.concatenate([jnp.where(lo, qp, zero), jnp.where(lo, zero, qp)], axis=0)
        s_nb = _dot_nt(qs, kb[:, sl]) + bias_ref[0, j]
        s_c = _dot_nt(qs, kc[:, sl])
        m = jnp.maximum(jnp.max(s_nb, axis=-1, keepdims=True), jnp.max(s_c, axis=-1, keepdims=True))
        p_nb = jnp.exp2(s_nb - m)
        p_c = jnp.exp2(s_c - m)
        l = jnp.sum(p_nb, axis=-1, keepdims=True) + jnp.sum(p_c, axis=-1, keepdims=True)
        o = (_dot(p_nb.astype(BF16), vb[:, sl]) + _dot(p_c.astype(BF16), vc[:, sl])) * (1.0 / l)
        o_ref[0, 0, :, sl] = jnp.where(lo, o[:GRID_W], o[GRID_W:]).astype(BF16)


def _na_attention(q, k, v, bias):
    rows_all = T // GRID_W
    q4, k4, v4 = (a.reshape(B, rows_all, GRID_W, D) for a in (q, k, v))

    def band_idx(b, r):
        return (b, CTX_ROWS + _band_start(jnp.maximum(r - CTX_ROWS, 0)), 0, 0)

    def bias_idx(b, r):
        rr = r - CTX_ROWS
        return (jnp.where(rr < 0, C_WIN_H, rr - _band_start(rr)), 0, 0, 0)

    band = pl.BlockSpec((pl.Element(1), pl.Element(C_WIN_H), pl.Element(GRID_W), pl.Element(D)),
                        band_idx)
    ctx = pl.BlockSpec((1, CTX_ROWS, GRID_W, D), lambda b, r: (b, 0, 0, 0))
    row = pl.BlockSpec((1, 1, GRID_W, D), lambda b, r: (b, r, 0, 0))
    out = pl.pallas_call(
        _na_kernel,
        grid=(B, rows_all),
        in_specs=[row, band, band, ctx, ctx,
                  pl.BlockSpec((1, N_PAIRS, 2 * GRID_W, BAND), bias_idx)],
        out_specs=row,
        out_shape=jax.ShapeDtypeStruct((B, rows_all, GRID_W, D), BF16),
        compiler_params=_cparams(2),
        name="na_attn",
    )(q4, k4, v4, k4, v4, bias)
    return out.reshape(B, T, D)


def kernel(x, c, ctx, c_ctx, w_mod, b_mod, norm_g, w_ffn_gate, w_ffn_up, w_ffn_down, a_w_in, a_q_norm, a_kv_norm, a_w_uq, a_w_ukv, a_w_out, b_w_qkv, b_lambda_q1, b_lambda_k1, b_lambda_q2, b_lambda_k2, b_subln, b_w_out, c_w_qkv, c_rpb, c_w_out, final_g):
    s = jnp.concatenate([ctx, x], axis=1)
    cond = jnp.concatenate([c, c_ctx[None, :], jnp.zeros((8 - B - 1, D), F32)], axis=0)
    mods = _modulation(cond, w_mod, b_mod).reshape(DEPTH, 8, N_MOD, D)
    rope = _rope_tables()
    wg, wu, wd = (w.astype(BF16) for w in (w_ffn_gate, w_ffn_up, w_ffn_down))

    for i in range(DEPTH):
        g = norm_g[i].reshape(3, 1, D)
        s = _ffn(s, mods[i], g[0], wg[i, 0], wu[i, 0], wd[i, 0], base=0)
        kind, j = i % 3, i // 3
        if kind == 0:
            q, k, v = _mla_proj(s, mods[i], g[1], a_w_in[j], a_q_norm[j], a_kv_norm[j],
                                a_w_uq[j], a_w_ukv[j], rope)
            o = _attention(q, k, v, mode="mla", n_heads=A_HEADS, dk=2 * LANES)
            w_out = a_w_out[j]
        elif kind == 1:
            lambda_init = 0.8 - 0.6 * math.exp(-0.3 * i)
            q, k, v = _qkv_proj(s, mods[i], g[1], b_w_qkv[j], rope,
                                q_scale=B_HEAD ** -0.5 * LOG2E, rope=True)
            lam = jnp.stack([b_lambda_q1[j], b_lambda_k1[j], b_lambda_q2[j], b_lambda_k2[j]])
            o = _attention(q, k, v, mode="diff", n_heads=B_HEADS, dk=LANES, lam=lam,
                           gsub=b_subln[j].reshape(1, -1), lambda_init=lambda_init)
            w_out = b_w_out[j]
        else:
            q, k, v = _qkv_proj(s, mods[i], g[1], c_w_qkv[j], rope,
                                q_scale=C_HEAD ** -0.5 * LOG2E, rope=False)
            o = _na_attention(q, k, v, _na_bias(c_rpb[j]))
            w_out = c_w_out[j]
        s = _ffn(s, mods[i], g[2], wg[i, 1], wu[i, 1], wd[i, 1], base=6,
                 mix=(o, w_out.astype(BF16)),
                 final_g=final_g.reshape(1, D) if i == DEPTH - 1 else None)
    return s
```

```python
import functools
import math

import jax
import jax.numpy as jnp
from jax import lax
from jax.experimental import pallas as pl
from jax.experimental.pallas import tpu as pltpu

F32 = jnp.float32
BF16 = jnp.bfloat16

D = 1024
B = 2
SEQ = 8192
CTX = 256
T = SEQ + CTX
DEPTH = 4
GRID_W = 64
D_FF = 2816
N_MOD = 9
EPS = 1e-6
ROPE_BASE = 10000.0
LOG2E = math.log2(math.e)
NEG = -1e30

A_HEADS, A_Q_LORA, A_KV_LORA, A_NOPE, A_ROPE, A_V = 8, 256, 128, 128, 64, 128
B_HEADS, B_HEAD = 8, 64
C_HEADS, C_HEAD, C_WIN_H, C_WIN_W = 16, 64, 8, 16

LANES = 128
TM = 256
NT = T // TM
ATT_COLS = 512
TK = 1024
VMEM_LIMIT = 56 * 1024 * 1024

assert CTX == TM and SEQ % TK == 0 and SEQ % TM == 0


def _cparams(n_grid):
    return pltpu.CompilerParams(dimension_semantics=("arbitrary",) * n_grid,
                                vmem_limit_bytes=VMEM_LIMIT)


def _dot(a, b):
    return jnp.dot(a, b, preferred_element_type=F32)


def _dot_nt(a, b):
    return lax.dot_general(a, b, (((1,), (1,)), ((), ())), preferred_element_type=F32)


def _dot_tn(a, b):
    return lax.dot_general(a, b, (((0,), (0,)), ((), ())), preferred_element_type=F32)


def _rms(x, g):
    return x * lax.rsqrt(jnp.mean(x * x, axis=-1, keepdims=True) + EPS) * g


def _modulated(x, g, shift, scale):
    return _rms(x, g) * (1.0 + scale) + shift


def _silu(x):
    return x * (1.0 / (1.0 + jnp.exp(-x)))


def _mod_row_index(b, t):
    return jnp.where(t == NT - 1, B, b)


def _mod_kernel(c_ref, w_ref, b_ref, o_ref):
    a = _silu(c_ref[...])
    o_ref[0] = jnp.dot(a, w_ref[0], preferred_element_type=F32,
                       precision=lax.Precision.HIGHEST) + b_ref[0]


def _modulation(cond, w_mod, b_mod):
    n_cols = N_MOD * D
    bn = D
    return pl.pallas_call(
        _mod_kernel,
        grid=(DEPTH, n_cols // bn),
        in_specs=[pl.BlockSpec((8, D), lambda l, n: (0, 0)),
                  pl.BlockSpec((1, D, bn), lambda l, n: (l, 0, n)),
                  pl.BlockSpec((1, 1, bn), lambda l, n: (l, 0, n))],
        out_specs=pl.BlockSpec((1, 8, bn), lambda l, n: (l, 0, n)),
        out_shape=jax.ShapeDtypeStruct((DEPTH, 8, n_cols), F32),
        compiler_params=_cparams(2),
        name="modulation",
    )(cond, w_mod, b_mod.reshape(DEPTH, 1, n_cols))


def _ffn_kernel(*refs, base, pre, has_ctx, final):
    refs = list(refs)
    s_ref = refs.pop(0)
    if pre:
        o_lat_ref = refs.pop(0)
        o_ctx_ref = refs.pop(0) if has_ctx else None
        wo_ref = refs.pop(0)
    mod_ref, g_ref, wg_ref, wu_ref, wd_ref = refs[:5]
    refs = refs[5:]
    if final:
        fg_ref = refs.pop(0)
    out_ref = refs.pop(0)

    x = s_ref[0]
    mod = mod_ref[0]
    if pre:
        o = o_lat_ref[0]
        if has_ctx:
            o = jnp.where(pl.program_id(1) == NT - 1, o_ctx_ref[0], o)
        x = x + mod[5:6] * _dot(o, wo_ref[...])
    h = _modulated(x, g_ref[...], mod[base:base + 1], mod[base + 1:base + 2]).astype(BF16)
    gate = _dot(h, wg_ref[...])
    up = _dot(h, wu_ref[...])
    a = (_silu(gate) * up).astype(BF16)
    y = _dot(a, wd_ref[...])
    out = x + (0.5 * mod[base + 2:base + 3]) * y
    if final:
        out = _rms(out, fg_ref[...])
    out_ref[0] = out


def _ffn(s, mods_l, g, wg, wu, wd, *, base, mix=None, final_g=None):
    pre, final = mix is not None, final_g is not None
    has_ctx = pre and mix[1] is not None
    resident = functools.partial(pl.BlockSpec, pipeline_mode=pl.Buffered(1))
    tile = pl.BlockSpec((1, TM, D), lambda b, t: (b, t, 0))
    args, specs = [s], [tile]
    if pre:
        args.append(mix[0])
        specs.append(pl.BlockSpec((1, TM, D), lambda b, t: (b, jnp.minimum(t, SEQ // TM - 1), 0)))
        if has_ctx:
            ctx_tile = mix[1].shape[1] // TM - 1
            args.append(mix[1])
            specs.append(pl.BlockSpec((1, TM, D), lambda b, t: (b, ctx_tile, 0)))
        args.append(mix[2])
        specs.append(resident((D, D), lambda b, t: (0, 0)))
    args += [mods_l, g, wg, wu, wd]
    specs += [pl.BlockSpec((1, N_MOD, D), lambda b, t: (_mod_row_index(b, t), 0, 0)),
              pl.BlockSpec((1, D), lambda b, t: (0, 0)),
              resident((D, D_FF), lambda b, t: (0, 0)),
              resident((D, D_FF), lambda b, t: (0, 0)),
              resident((D_FF, D), lambda b, t: (0, 0))]
    if final:
        args.append(final_g)
        specs.append(pl.BlockSpec((1, D), lambda b, t: (0, 0)))
    n_tiles = SEQ // TM if final else NT
    return pl.pallas_call(
        functools.partial(_ffn_kernel, base=base, pre=pre, has_ctx=has_ctx, final=final),
        grid=(B, n_tiles),
        in_specs=specs,
        out_specs=tile,
        out_shape=jax.ShapeDtypeStruct((B, n_tiles * TM, D), F32),
        compiler_params=_cparams(2),
        name="ffn_final" if final else ("ffn_mix" if pre else "ffn"),
    )(*args)


def _rope_tables():
    t = jnp.arange(SEQ, dtype=jnp.int32)
    row = (t // GRID_W).astype(F32)
    col = (t % GRID_W).astype(F32)
    axis_dim = A_ROPE // 2
    inv = ROPE_BASE ** (-jnp.arange(0, axis_dim, 2, dtype=F32) / axis_dim)
    ang = jnp.concatenate([row[:, None] * inv, col[:, None] * inv], axis=-1)
    cos = jnp.concatenate([jnp.cos(ang), jnp.ones((CTX, 32), F32)], axis=0)
    sin = jnp.concatenate([jnp.sin(ang), jnp.zeros((CTX, 32), F32)], axis=0)
    cos = jnp.tile(cos, (1, 4))
    sin = jnp.tile(sin, (1, 4))
    first_half = (jnp.arange(LANES) % 64) < 32
    return cos, jnp.where(first_half, -sin, 0.0), jnp.where(first_half, 0.0, sin)


def _rope_slab(x, cc, sa, sb):
    return x * cc + pltpu.roll(x, 96, 1) * sa + pltpu.roll(x, 32, 1) * sb


def _mla_proj_kernel(s_ref, mod_ref, g_ref, wq_ref, wkv_ref, wpe_ref, gq_ref, gkv_ref,
                     wuqn_ref, wuqp_ref, wuk_ref, wuv_ref, cc_ref, sa_ref, sb_ref,
                     q_ref, k_ref, v_ref, *, q_scale):
    mod = mod_ref[0]
    h = _modulated(s_ref[0], g_ref[...], mod[3:4], mod[4:5]).astype(BF16)
    cc, sa, sb = cc_ref[...], sa_ref[...], sb_ref[...]
    cq = _rms(_dot(h, wq_ref[...]), gq_ref[...]).astype(BF16)
    ckv = _rms(_dot(h, wkv_ref[...]), gkv_ref[...]).astype(BF16)
    kpe = _dot(h, wpe_ref[...])
    kpe = [_rope_slab(kpe[:, i * LANES:(i + 1) * LANES], cc, sa, sb).astype(BF16) for i in range(2)]
    qn = _dot(cq, wuqn_ref[...]) * q_scale
    qp = _dot(cq, wuqp_ref[...])
    qp = [(_rope_slab(qp[:, j * LANES:(j + 1) * LANES], cc, sa, sb) * q_scale).astype(BF16)
          for j in range(A_HEADS // 2)]
    kn = _dot(ckv, wuk_ref[...])
    v_ref[0] = _dot(ckv, wuv_ref[...]).astype(BF16)
    for hd in range(A_HEADS):
        lo = 2 * LANES * hd
        q_ref[0, :, lo:lo + LANES] = qn[:, hd * LANES:(hd + 1) * LANES].astype(BF16)
        q_ref[0, :, lo + LANES:lo + 2 * LANES] = qp[hd // 2]
        k_ref[0, :, lo:lo + LANES] = kn[:, hd * LANES:(hd + 1) * LANES].astype(BF16)
        k_ref[0, :, lo + LANES:lo + 2 * LANES] = kpe[hd % 2]


def _mla_proj(s, mods_l, g, w_in, g_q, g_kv, w_uq, w_ukv, rope):
    wq = w_in[:, :A_Q_LORA].astype(BF16)
    wkv = w_in[:, A_Q_LORA:A_Q_LORA + A_KV_LORA].astype(BF16)
    wpe = w_in[:, A_Q_LORA + A_KV_LORA:]
    z = jnp.zeros_like(wpe)
    wpe = jnp.concatenate([wpe, z, z, wpe], axis=1).astype(BF16)
    wuq = w_uq.reshape(A_Q_LORA, A_HEADS, A_NOPE + A_ROPE)
    wuqn = wuq[:, :, :A_NOPE].reshape(A_Q_LORA, A_HEADS * A_NOPE).astype(BF16)
    wuqp = wuq[:, :, A_NOPE:].reshape(A_Q_LORA, A_HEADS * A_ROPE).astype(BF16)
    wukv = w_ukv.reshape(A_KV_LORA, A_HEADS, A_NOPE + A_V)
    wuk = wukv[:, :, :A_NOPE].reshape(A_KV_LORA, A_HEADS * A_NOPE).astype(BF16)
    wuv = wukv[:, :, A_NOPE:].reshape(A_KV_LORA, A_HEADS * A_V).astype(BF16)
    q_scale = (A_NOPE + A_ROPE) ** -0.5 * LOG2E

    def full(shape):
        return pl.BlockSpec(shape, lambda b, t: (0,) * len(shape))

    tile = lambda w: pl.BlockSpec((1, TM, w), lambda b, t: (b, t, 0))
    tab = pl.BlockSpec((TM, LANES), lambda b, t: (t, 0))
    slot_w = 2 * LANES * A_HEADS
    return pl.pallas_call(
        functools.partial(_mla_proj_kernel, q_scale=q_scale),
        grid=(B, NT),
        in_specs=[tile(D),
                  pl.BlockSpec((1, N_MOD, D), lambda b, t: (_mod_row_index(b, t), 0, 0)),
                  full((1, D)), full(wq.shape), full(wkv.shape), full(wpe.shape),
                  full((1, A_Q_LORA)), full((1, A_KV_LORA)),
                  full(wuqn.shape), full(wuqp.shape), full(wuk.shape), full(wuv.shape),
                  tab, tab, tab],
        out_specs=[tile(slot_w), tile(slot_w), tile(A_HEADS * A_V)],
        out_shape=[jax.ShapeDtypeStruct((B, T, slot_w), BF16),
                   jax.ShapeDtypeStruct((B, T, slot_w), BF16),
                   jax.ShapeDtypeStruct((B, T, A_HEADS * A_V), BF16)],
        compiler_params=_cparams(2),
        name="mla_proj",
    )(s, mods_l, g, wq, wkv, wpe, g_q.reshape(1, -1), g_kv.reshape(1, -1),
      wuqn, wuqp, wuk, wuv, *rope)


def _qkv_proj_kernel(s_ref, mod_ref, g_ref, wq_ref, wk_ref, wv_ref, cc_ref, sa_ref, sb_ref,
                     q_ref, k_ref, v_ref, *, q_scale, rope):
    mod = mod_ref[0]
    h = _modulated(s_ref[0], g_ref[...], mod[3:4], mod[4:5]).astype(BF16)
    q = _dot(h, wq_ref[...])
    k = _dot(h, wk_ref[...])
    v_ref[0] = _dot(h, wv_ref[...]).astype(BF16)
    if rope:
        cc, sa, sb = cc_ref[...], sa_ref[...], sb_ref[...]
        for j in range(D // LANES):
            sl = slice(j * LANES, (j + 1) * LANES)
            q_ref[0, :, sl] = (_rope_slab(q[:, sl], cc, sa, sb) * q_scale).astype(BF16)
            k_ref[0, :, sl] = _rope_slab(k[:, sl], cc, sa, sb).astype(BF16)
    else:
        q_ref[0] = (q * q_scale).astype(BF16)
        k_ref[0] = k.astype(BF16)


def _qkv_proj(s, mods_l, g, w_qkv, rope_tabs, *, q_scale, rope):
    wq = w_qkv[:, :D].astype(BF16)
    wk = w_qkv[:, D:2 * D].astype(BF16)
    wv = w_qkv[:, 2 * D:].astype(BF16)
    full = lambda shape: pl.BlockSpec(shape, lambda b, t: (0,) * len(shape))
    tile = pl.BlockSpec((1, TM, D), lambda b, t: (b, t, 0))
    tab = pl.BlockSpec((TM, LANES), lambda b, t: (t, 0))
    return pl.pallas_call(
        functools.partial(_qkv_proj_kernel, q_scale=q_scale, rope=rope),
        grid=(B, NT),
        in_specs=[tile, pl.BlockSpec((1, N_MOD, D), lambda b, t: (_mod_row_index(b, t), 0, 0)),
                  full((1, D)), full((D, D)), full((D, D)), full((D, D)), tab, tab, tab],
        out_specs=[tile, tile, tile],
        out_shape=[jax.ShapeDtypeStruct((B, T, D), BF16)] * 3,
        compiler_params=_cparams(2),
        name="qkv_proj_rope" if rope else "qkv_proj",
    )(s, mods_l, g, wq, wk, wv, *rope_tabs)


def _attn_kernel(*refs, mode, lambda_init, ctx_start, n_latent_chunks):
    if mode == "diff":
        q_ref, k_ref, v_ref, lam_ref, gsub_ref, o_ref = refs
    else:
        q_ref, k_ref, v_ref, o_ref = refs
    q = q_ref[0]
    if mode == "diff":
        lane = lax.broadcasted_iota(jnp.int32, q.shape, 1)
        zero = jnp.zeros_like(q)
        q = jnp.concatenate([jnp.where(lane < B_HEAD, q, zero),
                             jnp.where(lane >= B_HEAD, q, zero)], axis=0)

    def chunk(start, size, carry):
        k = k_ref[0, pl.ds(start, size), :]
        v = v_ref[0, pl.ds(start, size), :]
        st = _dot_nt(k, q)
        m_new = jnp.max(st, axis=0, keepdims=True)
        if carry is not None:
            m, l, acc = carry
            m_new = jnp.maximum(m, m_new)
            alpha = jnp.exp2(m - m_new)
        p = jnp.exp2(st - m_new)
        l_new = jnp.sum(p, axis=0, keepdims=True)
        pv = _dot_tn(v, p.astype(BF16))
        if carry is not None:
            l_new = alpha * l + l_new
            pv = alpha * acc + pv
        return m_new, l_new, pv

    carry = chunk(ctx_start, CTX, None)
    if n_latent_chunks:
        carry = lax.fori_loop(
            0, n_latent_chunks,
            lambda c, cr: chunk(pl.multiple_of(c * TK, TK), TK, cr), carry)
    _, l, acc = carry
    o = (acc * (1.0 / l)).T
    if mode == "diff":
        f = lam_ref[...]
        lam = (jnp.exp(jnp.sum(f[0:1] * f[1:2], axis=-1, keepdims=True))
               - jnp.exp(jnp.sum(f[2:3] * f[3:4], axis=-1, keepdims=True)) + lambda_init)
        half = o.shape[0] // 2
        o = o[:half] - lam * o[half:]
        o = _rms(o, gsub_ref[...]) * (1.0 - lambda_init)
    o_ref[0] = o.astype(BF16)


def _attention(q, k, v, *, mode, n_heads, dk, latent, lam=None, gsub=None, lambda_init=0.0):
    dv = LANES
    n_maps = 2 if mode == "diff" else 1
    if latent:
        tq = ATT_COLS // n_maps
        n_q, q0, kv_rows, kv0 = SEQ // tq, 0, T, 0
    else:
        tq = CTX
        n_q, q0, kv_rows, kv0 = 1, SEQ // CTX, CTX, SEQ // CTX
    args = [q, k, v]
    specs = [pl.BlockSpec((1, tq, dk), lambda b, h, i: (b, q0 + i, h)),
             pl.BlockSpec((1, kv_rows, dk), lambda b, h, i: (b, kv0, h)),
             pl.BlockSpec((1, kv_rows, dv), lambda b, h, i: (b, kv0, h))]
    if mode == "diff":
        args += [lam, gsub]
        specs += [pl.BlockSpec(lam.shape, lambda b, h, i: (0, 0)),
                  pl.BlockSpec(gsub.shape, lambda b, h, i: (0, 0))]
    return pl.pallas_call(
        functools.partial(_attn_kernel, mode=mode, lambda_init=lambda_init,
                          ctx_start=SEQ if latent else 0,
                          n_latent_chunks=SEQ // TK if latent else 0),
        grid=(B, n_heads, n_q),
        in_specs=specs,
        out_specs=pl.BlockSpec((1, tq, dv), lambda b, h, i: (b, i, h)),
        out_shape=jax.ShapeDtypeStruct((B, n_q * tq, n_heads * dv), BF16),
        compiler_params=_cparams(3),
        name="attn_" + mode + ("" if latent else "_ctx"),
    )(*args)


N_ROWS = SEQ // GRID_W
CTX_ROWS = CTX // GRID_W
BAND = C_WIN_H * GRID_W
N_BIAS_TYPES = C_WIN_H + 1
N_PAIRS = C_HEADS // 2
RPB_H, RPB_W = 2 * C_WIN_H - 1, 2 * C_WIN_W - 1


def _band_start(rr):
    return jnp.clip(rr - C_WIN_H // 2, 0, N_ROWS - C_WIN_H)


def _na_bias_kernel(rpb_ref, o_ref):
    a = pl.program_id(0)
    j = pl.program_id(1)
    a_eff = jnp.minimum(a, C_WIN_H - 1)
    shape = (GRID_W, LANES)
    c = lax.broadcasted_iota(jnp.int32, shape, 0)
    lane = lax.broadcasted_iota(jnp.int32, shape, 1)
    kc = lane % GRID_W
    hi_lane = lane >= GRID_W
    diff = kc - c + (C_WIN_W - 1)
    cs = jnp.clip(c - C_WIN_W // 2, 0, GRID_W - C_WIN_W)
    valid = (kc >= cs) & (kc < cs + C_WIN_W) & (a < C_WIN_H)
    for hh in range(2):
        head = 2 * j + hh
        for u in range(C_WIN_H // 2):
            dr_lo = (C_WIN_H - 1) - a_eff + 2 * u
            base_lo = (head * RPB_H + dr_lo) * RPB_W
            blk = jnp.zeros(shape, F32)
            for jj in range(RPB_W):
                val = jnp.where(hi_lane, rpb_ref[base_lo + RPB_W + jj], rpb_ref[base_lo + jj])
                blk = jnp.where(diff == jj, val, blk)
            blk = jnp.where(valid, blk * LOG2E, NEG)
            o_ref[0, 0, hh * GRID_W:(hh + 1) * GRID_W, u * LANES:(u + 1) * LANES] = blk


def _na_bias(rpb):
    return pl.pallas_call(
        _na_bias_kernel,
        grid=(N_BIAS_TYPES, N_PAIRS),
        in_specs=[pl.BlockSpec(memory_space=pltpu.SMEM)],
        out_specs=pl.BlockSpec((1, 1, 2 * GRID_W, BAND), lambda a, j: (a, j, 0, 0)),
        out_shape=jax.ShapeDtypeStruct((N_BIAS_TYPES, N_PAIRS, 2 * GRID_W, BAND), F32),
        compiler_params=_cparams(2),
        name="na_bias",
    )(rpb.reshape(-1))


def _na_kernel(q_ref, kb_ref, vb_ref, kc_ref, vc_ref, bias_ref, o_ref):
    q = q_ref[0, 0]
    kb = kb_ref[0].reshape(BAND, D)
    vb = vb_ref[0].reshape(BAND, D)
    kc = kc_ref[0].reshape(CTX, D)
    vc = vc_ref[0].reshape(CTX, D)
    lane = lax.broadcasted_iota(jnp.int32, (GRID_W, LANES), 1)
    lo = lane < C_HEAD
    for j in range(N_PAIRS):
        sl = slice(j * LANES, (j + 1) * LANES)
        qp = q[:, sl]
        zero = jnp.zeros_like(qp)
        qs = jnp.concatenate([jnp.where(lo, qp, zero), jnp.where(lo, zero, qp)], axis=0)
        s_nb = _dot_nt(qs, kb[:, sl]) + bias_ref[0, j]
        s_c = _dot_nt(qs, kc[:, sl])
        m = jnp.maximum(jnp.max(s_nb, axis=-1, keepdims=True), jnp.max(s_c, axis=-1, keepdims=True))
        p_nb = jnp.exp2(s_nb - m)
        p_c = jnp.exp2(s_c - m)
        l = jnp.sum(p_nb, axis=-1, keepdims=True) + jnp.sum(p_c, axis=-1, keepdims=True)
        o = (_dot(p_nb.astype(BF16), vb[:, sl]) + _dot(p_c.astype(BF16), vc[:, sl])) * (1.0 / l)
        o_ref[0, 0, :, sl] = jnp.where(lo, o[:GRID_W], o[GRID_W:]).astype(BF16)


def _na_attention(q, k, v, bias):
    n_steps = T // GRID_W
    q4, k4, v4 = (a.reshape(B, n_steps, GRID_W, D) for a in (q, k, v))

    def band_idx(b, r):
        return (b, _band_start(jnp.minimum(r, N_ROWS - 1)), 0, 0)

    def bias_idx(b, r):
        return (jnp.where(r >= N_ROWS, C_WIN_H, r - _band_start(r)), 0, 0, 0)

    band = pl.BlockSpec((pl.Element(1), pl.Element(C_WIN_H), pl.Element(GRID_W), pl.Element(D)),
                        band_idx)
    ctx = pl.BlockSpec((1, CTX_ROWS, GRID_W, D), lambda b, r: (b, N_ROWS // CTX_ROWS, 0, 0))
    row = pl.BlockSpec((1, 1, GRID_W, D), lambda b, r: (b, r, 0, 0))
    out = pl.pallas_call(
        _na_kernel,
        grid=(B, n_steps),
        in_specs=[row, band, band, ctx, ctx,
                  pl.BlockSpec((1, N_PAIRS, 2 * GRID_W, BAND), bias_idx)],
        out_specs=row,
        out_shape=jax.ShapeDtypeStruct((B, n_steps, GRID_W, D), BF16),
        compiler_params=_cparams(2),
        name="na_attn",
    )(q4, k4, v4, k4, v4, bias)
    return out.reshape(B, T, D)


def kernel(x, c, ctx, c_ctx, w_mod, b_mod, norm_g, w_ffn_gate, w_ffn_up, w_ffn_down, a_w_in, a_q_norm, a_kv_norm, a_w_uq, a_w_ukv, a_w_out, b_w_qkv, b_lambda_q1, b_lambda_k1, b_lambda_q2, b_lambda_k2, b_subln, b_w_out, c_w_qkv, c_rpb, c_w_out, final_g):
    s = jnp.concatenate([x, ctx], axis=1)
    cond = jnp.concatenate([c, c_ctx[None, :], jnp.zeros((8 - B - 1, D), F32)], axis=0)
    mods = _modulation(cond, w_mod, b_mod).reshape(DEPTH, 8, N_MOD, D)
    rope = _rope_tables()
    wg, wu, wd = (w.astype(BF16) for w in (w_ffn_gate, w_ffn_up, w_ffn_down))

    for i in range(DEPTH):
        need_ctx_out = i < DEPTH - 1
        g = norm_g[i].reshape(3, 1, D)
        s = _ffn(s, mods[i], g[0], wg[i, 0], wu[i, 0], wd[i, 0], base=0)
        kind, j = i % 3, i // 3
        if kind == 0:
            q, k, v = _mla_proj(s, mods[i], g[1], a_w_in[j], a_q_norm[j], a_kv_norm[j],
                                a_w_uq[j], a_w_ukv[j], rope)
            att = functools.partial(_attention, q, k, v, mode="mla", n_heads=A_HEADS, dk=2 * LANES)
            o_lat = att(latent=True)
            o_ctx = att(latent=False) if need_ctx_out else None
            w_out = a_w_out[j]
        elif kind == 1:
            lambda_init = 0.8 - 0.6 * math.exp(-0.3 * i)
            q, k, v = _qkv_proj(s, mods[i], g[1], b_w_qkv[j], rope,
                                q_scale=B_HEAD ** -0.5 * LOG2E, rope=True)
            lam = jnp.stack([b_lambda_q1[j], b_lambda_k1[j], b_lambda_q2[j], b_lambda_k2[j]])
            att = functools.partial(_attention, q, k, v, mode="diff", n_heads=B_HEADS, dk=LANES,
                                    lam=lam, gsub=b_subln[j].reshape(1, -1),
                                    lambda_init=lambda_init)
            o_lat = att(latent=True)
            o_ctx = att(latent=False) if need_ctx_out else None
            w_out = b_w_out[j]
        else:
            q, k, v = _qkv_proj(s, mods[i], g[1], c_w_qkv[j], rope,
                                q_scale=C_HEAD ** -0.5 * LOG2E, rope=False)
            o_lat = o_ctx = _na_attention(q, k, v, _na_bias(c_rpb[j]))
            w_out = c_w_out[j]
        s = _ffn(s, mods[i], g[2], wg[i, 1], wu[i, 1], wd[i, 1], base=6,
                 mix=(o_lat, o_ctx, w_out.astype(BF16)),
                 final_g=final_g.reshape(1, D) if i == DEPTH - 1 else None)
    return s
```

```python
import functools
import math

import jax
import jax.numpy as jnp
from jax import lax
from jax.experimental import pallas as pl
from jax.experimental.pallas import tpu as pltpu

F32 = jnp.float32
BF16 = jnp.bfloat16

D = 1024
B = 2
SEQ = 8192
CTX = 256
T = SEQ + CTX
DEPTH = 4
GRID_W = 64
D_FF = 2816
N_MOD = 9
EPS = 1e-6
ROPE_BASE = 10000.0
LOG2E = math.log2(math.e)
NEG = -1e30

A_HEADS, A_Q_LORA, A_KV_LORA, A_NOPE, A_ROPE, A_V = 8, 256, 128, 128, 64, 128
B_HEADS, B_HEAD = 8, 64
C_HEADS, C_HEAD, C_WIN_H, C_WIN_W = 16, 64, 8, 16

LANES = 128
TM = 256
NT = T // TM
ATT_COLS = 512
TK = 1024
VMEM_LIMIT = 56 * 1024 * 1024

assert CTX == TM and SEQ % TK == 0 and SEQ % TM == 0


def _cparams(n_grid):
    return pltpu.CompilerParams(dimension_semantics=("arbitrary",) * n_grid,
                                vmem_limit_bytes=VMEM_LIMIT)


def _dot(a, b):
    return jnp.dot(a, b, preferred_element_type=F32)


def _dot_nt(a, b):
    return lax.dot_general(a, b, (((1,), (1,)), ((), ())), preferred_element_type=F32)


def _dot_tn(a, b):
    return lax.dot_general(a, b, (((0,), (0,)), ((), ())), preferred_element_type=F32)


def _rms(x, g):
    return x * lax.rsqrt(jnp.mean(x * x, axis=-1, keepdims=True) + EPS) * g


def _modulated(x, g, shift, scale):
    return _rms(x, g) * (1.0 + scale) + shift


def _silu(x):
    return x * (1.0 / (1.0 + jnp.exp(-x)))


def _mod_row_index(b, t):
    return jnp.where(t == NT - 1, B, b)


def _mod_kernel(c_ref, w_ref, b_ref, o_ref):
    a = _silu(c_ref[...])
    o_ref[0] = jnp.dot(a, w_ref[0], preferred_element_type=F32,
                       precision=lax.Precision.HIGHEST) + b_ref[0]


def _modulation(cond, w_mod, b_mod):
    n_cols = N_MOD * D
    bn = D
    return pl.pallas_call(
        _mod_kernel,
        grid=(DEPTH, n_cols // bn),
        in_specs=[pl.BlockSpec((8, D), lambda l, n: (0, 0)),
                  pl.BlockSpec((1, D, bn), lambda l, n: (l, 0, n)),
                  pl.BlockSpec((1, 1, bn), lambda l, n: (l, 0, n))],
        out_specs=pl.BlockSpec((1, 8, bn), lambda l, n: (l, 0, n)),
        out_shape=jax.ShapeDtypeStruct((DEPTH, 8, n_cols), F32),
        compiler_params=_cparams(2),
        name="modulation",
    )(cond, w_mod, b_mod.reshape(DEPTH, 1, n_cols))


def _ffn_kernel(*refs, base, pre, has_ctx, final):
    refs = list(refs)
    s_ref = refs.pop(0)
    if pre:
        o_lat_ref = refs.pop(0)
        o_ctx_ref = refs.pop(0) if has_ctx else None
        wo_ref = refs.pop(0)
    mod_ref, g_ref, wg_ref, wu_ref, wd_ref = refs[:5]
    refs = refs[5:]
    if final:
        fg_ref = refs.pop(0)
    out_ref = refs.pop(0)

    x = s_ref[0]
    mod = mod_ref[0]
    if pre:
        o = o_lat_ref[0]
        if has_ctx:
            o = jnp.where(pl.program_id(1) == NT - 1, o_ctx_ref[0], o)
        x = x + mod[5:6] * _dot(o, wo_ref[...])
    h = _modulated(x, g_ref[...], mod[base:base + 1], mod[base + 1:base + 2]).astype(BF16)
    gate = _dot(h, wg_ref[...])
    up = _dot(h, wu_ref[...])
    a = (_silu(gate) * up).astype(BF16)
    y = _dot(a, wd_ref[...])
    out = x + (0.5 * mod[base + 2:base + 3]) * y
    if final:
        out = _rms(out, fg_ref[...])
    out_ref[0] = out


def _ffn(s, mods_l, g, wg, wu, wd, *, base, mix=None, final_g=None):
    pre, final = mix is not None, final_g is not None
    has_ctx = pre and mix[1] is not None
    resident = functools.partial(pl.BlockSpec, pipeline_mode=pl.Buffered(1))
    tile = pl.BlockSpec((1, TM, D), lambda b, t: (b, t, 0))
    args, specs = [s], [tile]
    if pre:
        args.append(mix[0])
        specs.append(pl.BlockSpec((1, TM, D), lambda b, t: (b, jnp.minimum(t, SEQ // TM - 1), 0)))
        if has_ctx:
            ctx_tile = mix[1].shape[1] // TM - 1
            args.append(mix[1])
            specs.append(pl.BlockSpec((1, TM, D), lambda b, t: (b, ctx_tile, 0)))
        args.append(mix[2])
        specs.append(resident((D, D), lambda b, t: (0, 0)))
    args += [mods_l, g, wg, wu, wd]
    specs += [pl.BlockSpec((1, N_MOD, D), lambda b, t: (_mod_row_index(b, t), 0, 0)),
              pl.BlockSpec((1, D), lambda b, t: (0, 0)),
              resident((D, D_FF), lambda b, t: (0, 0)),
              resident((D, D_FF), lambda b, t: (0, 0)),
              resident((D_FF, D), lambda b, t: (0, 0))]
    if final:
        args.append(final_g)
        specs.append(pl.BlockSpec((1, D), lambda b, t: (0, 0)))
    n_tiles = SEQ // TM if final else NT
    return pl.pallas_call(
        functools.partial(_ffn_kernel, base=base, pre=pre, has_ctx=has_ctx, final=final),
        grid=(B, n_tiles),
        in_specs=specs,
        out_specs=tile,
        out_shape=jax.ShapeDtypeStruct((B, n_tiles * TM, D), F32),
        compiler_params=_cparams(2),
        name="ffn_final" if final else ("ffn_mix" if pre else "ffn"),
    )(*args)


def _rope_tables():
    t = jnp.arange(SEQ, dtype=jnp.int32)
    row = (t // GRID_W).astype(F32)
    col = (t % GRID_W).astype(F32)
    axis_dim = A_ROPE // 2
    inv = ROPE_BASE ** (-jnp.arange(0, axis_dim, 2, dtype=F32) / axis_dim)
    ang = jnp.concatenate([row[:, None] * inv, col[:, None] * inv], axis=-1)
    cos = jnp.concatenate([jnp.cos(ang), jnp.ones((CTX, 32), F32)], axis=0)
    sin = jnp.concatenate([jnp.sin(ang), jnp.zeros((CTX, 32), F32)], axis=0)
    cos = jnp.tile(cos, (1, 4))
    sin = jnp.tile(sin, (1, 4))
    first_half = (jnp.arange(LANES) % 64) < 32
    return cos, jnp.where(first_half, -sin, 0.0), jnp.where(first_half, 0.0, sin)


def _rope_slab(x, cc, sa, sb):
    return x * cc + pltpu.roll(x, 96, 1) * sa + pltpu.roll(x, 32, 1) * sb


def _mla_proj_kernel(s_ref, mod_ref, g_ref, wq_ref, wkv_ref, wpe_ref, gq_ref, gkv_ref,
                     wuqn_ref, wuqp_ref, wuk_ref, wuv_ref, cc_ref, sa_ref, sb_ref,
                     q_ref, k_ref, v_ref, *, q_scale):
    mod = mod_ref[0]
    h = _modulated(s_ref[0], g_ref[...], mod[3:4], mod[4:5]).astype(BF16)
    cc, sa, sb = cc_ref[...], sa_ref[...], sb_ref[...]
    cq = _rms(_dot(h, wq_ref[...]), gq_ref[...]).astype(BF16)
    ckv = _rms(_dot(h, wkv_ref[...]), gkv_ref[...]).astype(BF16)
    kpe = _dot(h, wpe_ref[...])
    kpe = [_rope_slab(kpe[:, i * LANES:(i + 1) * LANES], cc, sa, sb).astype(BF16) for i in range(2)]
    qn = _dot(cq, wuqn_ref[...]) * q_scale
    qp = _dot(cq, wuqp_ref[...])
    qp = [(_rope_slab(qp[:, j * LANES:(j + 1) * LANES], cc, sa, sb) * q_scale).astype(BF16)
          for j in range(A_HEADS // 2)]
    kn = _dot(ckv, wuk_ref[...])
    v_ref[0] = _dot(ckv, wuv_ref[...]).astype(BF16)
    for hd in range(A_HEADS):
        lo = 2 * LANES * hd
        q_ref[0, :, lo:lo + LANES] = qn[:, hd * LANES:(hd + 1) * LANES].astype(BF16)
        q_ref[0, :, lo + LANES:lo + 2 * LANES] = qp[hd // 2]
        k_ref[0, :, lo:lo + LANES] = kn[:, hd * LANES:(hd + 1) * LANES].astype(BF16)
        k_ref[0, :, lo + LANES:lo + 2 * LANES] = kpe[hd % 2]


def _mla_proj(s, mods_l, g, w_in, g_q, g_kv, w_uq, w_ukv, rope):
    wq = w_in[:, :A_Q_LORA].astype(BF16)
    wkv = w_in[:, A_Q_LORA:A_Q_LORA + A_KV_LORA].astype(BF16)
    wpe = w_in[:, A_Q_LORA + A_KV_LORA:]
    z = jnp.zeros_like(wpe)
    wpe = jnp.concatenate([wpe, z, z, wpe], axis=1).astype(BF16)
    wuq = w_uq.reshape(A_Q_LORA, A_HEADS, A_NOPE + A_ROPE)
    wuqn = wuq[:, :, :A_NOPE].reshape(A_Q_LORA, A_HEADS * A_NOPE).astype(BF16)
    wuqp = wuq[:, :, A_NOPE:].reshape(A_Q_LORA, A_HEADS * A_ROPE).astype(BF16)
    wukv = w_ukv.reshape(A_KV_LORA, A_HEADS, A_NOPE + A_V)
    wuk = wukv[:, :, :A_NOPE].reshape(A_KV_LORA, A_HEADS * A_NOPE).astype(BF16)
    wuv = wukv[:, :, A_NOPE:].reshape(A_KV_LORA, A_HEADS * A_V).astype(BF16)
    q_scale = (A_NOPE + A_ROPE) ** -0.5 * LOG2E

    def full(shape):
        return pl.BlockSpec(shape, lambda b, t: (0,) * len(shape))

    tile = lambda w: pl.BlockSpec((1, TM, w), lambda b, t: (b, t, 0))
    tab = pl.BlockSpec((TM, LANES), lambda b, t: (t, 0))
    slot_w = 2 * LANES * A_HEADS
    return pl.pallas_call(
        functools.partial(_mla_proj_kernel, q_scale=q_scale),
        grid=(B, NT),
        in_specs=[tile(D),
                  pl.BlockSpec((1, N_MOD, D), lambda b, t: (_mod_row_index(b, t), 0, 0)),
                  full((1, D)), full(wq.shape), full(wkv.shape), full(wpe.shape),
                  full((1, A_Q_LORA)), full((1, A_KV_LORA)),
                  full(wuqn.shape), full(wuqp.shape), full(wuk.shape), full(wuv.shape),
                  tab, tab, tab],
        out_specs=[tile(slot_w), tile(slot_w), tile(A_HEADS * A_V)],
        out_shape=[jax.ShapeDtypeStruct((B, T, slot_w), BF16),
                   jax.ShapeDtypeStruct((B, T, slot_w), BF16),
                   jax.ShapeDtypeStruct((B, T, A_HEADS * A_V), BF16)],
        compiler_params=_cparams(2),
        name="mla_proj",
    )(s, mods_l, g, wq, wkv, wpe, g_q.reshape(1, -1), g_kv.reshape(1, -1),
      wuqn, wuqp, wuk, wuv, *rope)


def _qkv_proj_kernel(s_ref, mod_ref, g_ref, wq_ref, wk_ref, wv_ref, cc_ref, sa_ref, sb_ref,
                     q_ref, k_ref, v_ref, *, q_scale, rope):
    mod = mod_ref[0]
    h = _modulated(s_ref[0], g_ref[...], mod[3:4], mod[4:5]).astype(BF16)
    q = _dot(h, wq_ref[...])
    k = _dot(h, wk_ref[...])
    v_ref[0] = _dot(h, wv_ref[...]).astype(BF16)
    if rope:
        cc, sa, sb = cc_ref[...], sa_ref[...], sb_ref[...]
        for j in range(D // LANES):
            sl = slice(j * LANES, (j + 1) * LANES)
            q_ref[0, :, sl] = (_rope_slab(q[:, sl], cc, sa, sb) * q_scale).astype(BF16)
            k_ref[0, :, sl] = _rope_slab(k[:, sl], cc, sa, sb).astype(BF16)
    else:
        q_ref[0] = (q * q_scale).astype(BF16)
        k_ref[0] = k.astype(BF16)


def _qkv_proj(s, mods_l, g, w_qkv, rope_tabs, *, q_scale, rope):
    wq = w_qkv[:, :D].astype(BF16)
    wk = w_qkv[:, D:2 * D].astype(BF16)
    wv = w_qkv[:, 2 * D:].astype(BF16)
    full = lambda shape: pl.BlockSpec(shape, lambda b, t: (0,) * len(shape))
    tile = pl.BlockSpec((1, TM, D), lambda b, t: (b, t, 0))
    tab = pl.BlockSpec((TM, LANES), lambda b, t: (t, 0))
    return pl.pallas_call(
        functools.partial(_qkv_proj_kernel, q_scale=q_scale, rope=rope),
        grid=(B, NT),
        in_specs=[tile, pl.BlockSpec((1, N_MOD, D), lambda b, t: (_mod_row_index(b, t), 0, 0)),
                  full((1, D)), full((D, D)), full((D, D)), full((D, D)), tab, tab, tab],
        out_specs=[tile, tile, tile],
        out_shape=[jax.ShapeDtypeStruct((B, T, D), BF16)] * 3,
        compiler_params=_cparams(2),
        name="qkv_proj_rope" if rope else "qkv_proj",
    )(s, mods_l, g, wq, wk, wv, *rope_tabs)


def _attn_kernel(*refs, mode, lambda_init, ctx_start, n_latent_chunks):
    if mode == "diff":
        q_ref, k_ref, v_ref, lam_ref, gsub_ref, o_ref, st0_ref, st1_ref = refs
    else:
        q_ref, k_ref, v_ref, o_ref, st0_ref, st1_ref = refs
    q = q_ref[0]
    if mode == "diff":
        lane = lax.broadcasted_iota(jnp.int32, q.shape, 1)
        zero = jnp.zeros_like(q)
        q = jnp.concatenate([jnp.where(lane < B_HEAD, q, zero),
                             jnp.where(lane >= B_HEAD, q, zero)], axis=0)
    cols = q.shape[0]
    dv = v_ref.shape[-1]

    def scores(start, size):
        return _dot_nt(k_ref[0, pl.ds(start, size), :], q)

    def update(st, start, size, carry):
        m, l, acc = carry
        v = v_ref[0, pl.ds(start, size), :]
        m_new = jnp.maximum(m, jnp.max(st, axis=0, keepdims=True))
        alpha = jnp.exp2(m - m_new)
        p = jnp.exp2(st - m_new)
        l = alpha * l + jnp.sum(p, axis=0, keepdims=True)
        acc = alpha * acc + _dot_tn(v, p.astype(BF16))
        return m_new, l, acc

    carry = (jnp.full((1, cols), -jnp.inf, F32), jnp.zeros((1, cols), F32),
             jnp.zeros((dv, cols), F32))
    st_refs = (st0_ref, st1_ref)
    if n_latent_chunks:
        st_refs[0][...] = scores(0, TK)
    for c in range(n_latent_chunks):
        if c + 1 < n_latent_chunks:
            st_refs[(c + 1) % 2][...] = scores((c + 1) * TK, TK)
        else:
            st_ctx = scores(ctx_start, CTX)
        carry = update(st_refs[c % 2][...], c * TK, TK, carry)
    if not n_latent_chunks:
        st_ctx = scores(ctx_start, CTX)
    _, l, acc = update(st_ctx, ctx_start, CTX, carry)
    o = (acc * (1.0 / l)).T
    if mode == "diff":
        f = lam_ref[...]
        lam = (jnp.exp(jnp.sum(f[0:1] * f[1:2], axis=-1, keepdims=True))
               - jnp.exp(jnp.sum(f[2:3] * f[3:4], axis=-1, keepdims=True)) + lambda_init)
        half = o.shape[0] // 2
        o = o[:half] - lam * o[half:]
        o = _rms(o, gsub_ref[...]) * (1.0 - lambda_init)
    o_ref[0] = o.astype(BF16)


def _attention(q, k, v, *, mode, n_heads, dk, latent, lam=None, gsub=None, lambda_init=0.0):
    dv = LANES
    n_maps = 2 if mode == "diff" else 1
    if latent:
        tq = ATT_COLS // n_maps
        n_q, q0, kv_rows, kv0 = SEQ // tq, 0, T, 0
    else:
        tq = CTX
        n_q, q0, kv_rows, kv0 = 1, SEQ // CTX, CTX, SEQ // CTX
    args = [q, k, v]
    specs = [pl.BlockSpec((1, tq, dk), lambda b, h, i: (b, q0 + i, h)),
             pl.BlockSpec((1, kv_rows, dk), lambda b, h, i: (b, kv0, h)),
             pl.BlockSpec((1, kv_rows, dv), lambda b, h, i: (b, kv0, h))]
    if mode == "diff":
        args += [lam, gsub]
        specs += [pl.BlockSpec(lam.shape, lambda b, h, i: (0, 0)),
                  pl.BlockSpec(gsub.shape, lambda b, h, i: (0, 0))]
    return pl.pallas_call(
        functools.partial(_attn_kernel, mode=mode, lambda_init=lambda_init,
                          ctx_start=SEQ if latent else 0,
                          n_latent_chunks=SEQ // TK if latent else 0),
        grid=(B, n_heads, n_q),
        in_specs=specs,
        out_specs=pl.BlockSpec((1, tq, dv), lambda b, h, i: (b, i, h)),
        out_shape=jax.ShapeDtypeStruct((B, n_q * tq, n_heads * dv), BF16),
        scratch_shapes=[pltpu.VMEM((TK, tq * n_maps), F32)] * 2,
        compiler_params=_cparams(3),
        name="attn_" + mode + ("" if latent else "_ctx"),
    )(*args)


N_ROWS = SEQ // GRID_W
CTX_ROWS = CTX // GRID_W
BAND = C_WIN_H * GRID_W
N_BIAS_TYPES = C_WIN_H + 1
N_PAIRS = C_HEADS // 2
RPB_H, RPB_W = 2 * C_WIN_H - 1, 2 * C_WIN_W - 1


def _band_start(rr):
    return jnp.clip(rr - C_WIN_H // 2, 0, N_ROWS - C_WIN_H)


def _na_bias_kernel(rpb_ref, o_ref):
    a = pl.program_id(0)
    j = pl.program_id(1)
    a_eff = jnp.minimum(a, C_WIN_H - 1)
    shape = (GRID_W, LANES)
    c = lax.broadcasted_iota(jnp.int32, shape, 0)
    lane = lax.broadcasted_iota(jnp.int32, shape, 1)
    kc = lane % GRID_W
    hi_lane = lane >= GRID_W
    diff = kc - c + (C_WIN_W - 1)
    cs = jnp.clip(c - C_WIN_W // 2, 0, GRID_W - C_WIN_W)
    valid = (kc >= cs) & (kc < cs + C_WIN_W) & (a < C_WIN_H)
    for hh in range(2):
        head = 2 * j + hh
        for u in range(C_WIN_H // 2):
            dr_lo = (C_WIN_H - 1) - a_eff + 2 * u
            base_lo = (head * RPB_H + dr_lo) * RPB_W
            blk = jnp.zeros(shape, F32)
            for jj in range(RPB_W):
                val = jnp.where(hi_lane, rpb_ref[base_lo + RPB_W + jj], rpb_ref[base_lo + jj])
                blk = jnp.where(diff == jj, val, blk)
            blk = jnp.where(valid, blk * LOG2E, NEG)
            o_ref[0, 0, hh * GRID_W:(hh + 1) * GRID_W, u * LANES:(u + 1) * LANES] = blk


def _na_bias(rpb):
    return pl.pallas_call(
        _na_bias_kernel,
        grid=(N_BIAS_TYPES, N_PAIRS),
        in_specs=[pl.BlockSpec(memory_space=pltpu.SMEM)],
        out_specs=pl.BlockSpec((1, 1, 2 * GRID_W, BAND), lambda a, j: (a, j, 0, 0)),
        out_shape=jax.ShapeDtypeStruct((N_BIAS_TYPES, N_PAIRS, 2 * GRID_W, BAND), F32),
        compiler_params=_cparams(2),
        name="na_bias",
    )(rpb.reshape(-1))


def _na_kernel(q_ref, kb_ref, vb_ref, kc_ref, vc_ref, bias_ref, o_ref):
    q = q_ref[0, 0]
    kb = kb_ref[0].reshape(BAND, D)
    vb = vb_ref[0].reshape(BAND, D)
    kc = kc_ref[0].reshape(CTX, D)
    vc = vc_ref[0].reshape(CTX, D)
    lane = lax.broadcasted_iota(jnp.int32, (GRID_W, LANES), 1)
    lo = lane < C_HEAD
    for j in range(N_PAIRS):
        sl = slice(j * LANES, (j + 1) * LANES)
        qp = q[:, sl]
        zero = jnp.zeros_like(qp)
        qs = jnp.concatenate([jnp.where(lo, qp, zero), jnp.where(lo, zero, qp)], axis=0)
        s_nb = _dot_nt(qs, kb[:, sl]) + bias_ref[0, j]
        s_c = _dot_nt(qs, kc[:, sl])
        m = jnp.maximum(jnp.max(s_nb, axis=-1, keepdims=True), jnp.max(s_c, axis=-1, keepdims=True))
        p_nb = jnp.exp2(s_nb - m)
        p_c = jnp.exp2(s_c - m)
        l = jnp.sum(p_nb, axis=-1, keepdims=True) + jnp.sum(p_c, axis=-1, keepdims=True)
        o = (_dot(p_nb.astype(BF16), vb[:, sl]) + _dot(p_c.astype(BF16), vc[:, sl])) * (1.0 / l)
        o_ref[0, 0, :, sl] = jnp.where(lo, o[:GRID_W], o[GRID_W:]).astype(BF16)


def _na_attention(q, k, v, bias):
    n_steps = T // GRID_W
    q4, k4, v4 = (a.reshape(B, n_steps, GRID_W, D) for a in (q, k, v))

    def band_idx(b, r):
        return (b, _band_start(jnp.minimum(r, N_ROWS - 1)), 0, 0)

    def bias_idx(b, r):
        return (jnp.where(r >= N_ROWS, C_WIN_H, r - _band_start(r)), 0, 0, 0)

    band = pl.BlockSpec((pl.Element(1), pl.Element(C_WIN_H), pl.Element(GRID_W), pl.Element(D)),
                        band_idx)
    ctx = pl.BlockSpec((1, CTX_ROWS, GRID_W, D), lambda b, r: (b, N_ROWS // CTX_ROWS, 0, 0))
    row = pl.BlockSpec((1, 1, GRID_W, D), lambda b, r: (b, r, 0, 0))
    out = pl.pallas_call(
        _na_kernel,
        grid=(B, n_steps),
        in_specs=[row, band, band, ctx, ctx,
                  pl.BlockSpec((1, N_PAIRS, 2 * GRID_W, BAND), bias_idx)],
        out_specs=row,
        out_shape=jax.ShapeDtypeStruct((B, n_steps, GRID_W, D), BF16),
        compiler_params=_cparams(2),
        name="na_attn",
    )(q4, k4, v4, k4, v4, bias)
    return out.reshape(B, T, D)


def kernel(x, c, ctx, c_ctx, w_mod, b_mod, norm_g, w_ffn_gate, w_ffn_up, w_ffn_down, a_w_in, a_q_norm, a_kv_norm, a_w_uq, a_w_ukv, a_w_out, b_w_qkv, b_lambda_q1, b_lambda_k1, b_lambda_q2, b_lambda_k2, b_subln, b_w_out, c_w_qkv, c_rpb, c_w_out, final_g):
    s = jnp.concatenate([x, ctx], axis=1)
    cond = jnp.concatenate([c, c_ctx[None, :], jnp.zeros((8 - B - 1, D), F32)], axis=0)
    mods = _modulation(cond, w_mod, b_mod).reshape(DEPTH, 8, N_MOD, D)
    rope = _rope_tables()
    wg, wu, wd = (w.astype(BF16) for w in (w_ffn_gate, w_ffn_up, w_ffn_down))

    for i in range(DEPTH):
        need_ctx_out = i < DEPTH - 1
        g = norm_g[i].reshape(3, 1, D)
        s = _ffn(s, mods[i], g[0], wg[i, 0], wu[i, 0], wd[i, 0], base=0)
        kind, j = i % 3, i // 3
        if kind == 0:
            q, k, v = _mla_proj(s, mods[i], g[1], a_w_in[j], a_q_norm[j], a_kv_norm[j],
                                a_w_uq[j], a_w_ukv[j], rope)
            att = functools.partial(_attention, q, k, v, mode="mla", n_heads=A_HEADS, dk=2 * LANES)
            o_lat = att(latent=True)
            o_ctx = att(latent=False) if need_ctx_out else None
            w_out = a_w_out[j]
        elif kind == 1:
            lambda_init = 0.8 - 0.6 * math.exp(-0.3 * i)
            q, k, v = _qkv_proj(s, mods[i], g[1], b_w_qkv[j], rope,
                                q_scale=B_HEAD ** -0.5 * LOG2E, rope=True)
            lam = jnp.stack([b_lambda_q1[j], b_lambda_k1[j], b_lambda_q2[j], b_lambda_k2[j]])
            att = functools.partial(_attention, q, k, v, mode="diff", n_heads=B_HEADS, dk=LANES,
                                    lam=lam, gsub=b_subln[j].reshape(1, -1),
                                    lambda_init=lambda_init)
            o_lat = att(latent=True)
            o_ctx = att(latent=False) if need_ctx_out else None
            w_out = b_w_out[j]
        else:
            q, k, v = _qkv_proj(s, mods[i], g[1], c_w_qkv[j], rope,
                                q_scale=C_HEAD ** -0.5 * LOG2E, rope=False)
            o_lat = o_ctx = _na_attention(q, k, v, _na_bias(c_rpb[j]))
            w_out = c_w_out[j]
        s = _ffn(s, mods[i], g[2], wg[i, 1], wu[i, 1], wd[i, 1], base=6,
                 mix=(o_lat, o_ctx, w_out.astype(BF16)),
                 final_g=final_g.reshape(1, D) if i == DEPTH - 1 else None)
    return s
```
